```python
import math
import jax
import jax.numpy as jnp
from jax import lax
import numpy as np

D_MODEL = 1024
BATCH = 16
SEQ = 256
DEPTH = 4
DEC_BATCH = 2
DEC_SEQ = 4096
PAST_LEN = 512

GRID_W = 64
MIX_W = D_MODEL
RWKV_W = MIX_W // 2
RWKV_HEAD = 64
RWKV_HEADS = RWKV_W // RWKV_HEAD
DECAY_LORA = 64
AAA_LORA = 64
GATE_LORA = 160
ATT_W = MIX_W - RWKV_W
ATT_HEAD = 64
ATT_HEADS = ATT_W // (2 * ATT_HEAD)
ROPE_AXIS = ATT_HEAD // 2
ROPE_THETA = 10000.0
Q_BLOCK = 128
N_EXPERTS = 16
N_GROUPS = 4
TOP_K = 2
EXPERT_FF = 512
NORM_EPS = 1e-6
GN_EPS = 64e-5
SPLIT_SIZES = (3 * RWKV_W, 2 * DECAY_LORA, 2 * AAA_LORA, GATE_LORA, ATT_W, ATT_W, ATT_W)
IN_W = 3 * RWKV_W + 2 * DECAY_LORA + 2 * AAA_LORA + GATE_LORA + 3 * ATT_W
F32 = jnp.float32

kernel_name = 'hybrid_rwkv7_diffattn_grouped_moe_dit_step'


def _split_points():
    return np.cumsum(np.array(SPLIT_SIZES))[:-1].tolist()


def rms_norm(x, g):
    xf = x.astype(F32)
    y = xf * lax.rsqrt(jnp.mean(xf * xf, axis=-1, keepdims=True) + NORM_EPS)
    return (y * g.astype(F32)).astype(x.dtype)


def head_group_norm(y, w, b):
    B, T, H, N = y.shape
    yf = y.astype(F32)
    mu = jnp.mean(yf, axis=-1, keepdims=True)
    var = jnp.mean(jnp.square(yf - mu), axis=-1, keepdims=True)
    yn = ((yf - mu) * lax.rsqrt(var + GN_EPS)).reshape(B, T, H * N)
    return yn * w.astype(F32) + b.astype(F32)


def centred_shift(x, w):
    xp = jnp.pad(x, ((0, 0), (1, 1), (0, 0)))
    return w[0] * xp[:, :-2] + w[1] * xp[:, 1:-1] + w[2] * xp[:, 2:]


def axial_rope(T):
    rows = T // GRID_W
    row = jnp.repeat(jnp.arange(rows), GRID_W)
    col = jnp.tile(jnp.arange(GRID_W), rows)
    inv = 1.0 / (ROPE_THETA ** (jnp.arange(0, ROPE_AXIS, 2, dtype=F32) / ROPE_AXIS))
    ang = jnp.concatenate([row[:, None] * inv, col[:, None] * inv], axis=-1)
    return jnp.cos(ang), jnp.sin(ang)


def apply_axial_rope(x, cos, sin):
    B, T, H, Dh = x.shape
    xr = x.astype(F32).reshape(B, T, H, 2, 2, ROPE_AXIS // 2)
    x1, x2 = xr[..., 0, :], xr[..., 1, :]
    c = cos.reshape(T, 2, ROPE_AXIS // 2)[None, :, None]
    s = sin.reshape(T, 2, ROPE_AXIS // 2)[None, :, None]
    out = jnp.stack([x1 * c - x2 * s, x1 * s + x2 * c], axis=-2)
    return out.reshape(B, T, H, Dh).astype(x.dtype)


def diff_attention(q1, q2, k1, k2, v, lam):
    B, T, H, Dh = q1.shape
    nb = T // Q_BLOCK
    scale = Dh ** -0.5
    k1f, k2f, vf = k1.astype(F32), k2.astype(F32), v.astype(F32)

    def blocks(t):
        return jnp.moveaxis(t.reshape(B, nb, Q_BLOCK, H, Dh), 1, 0)

    def one_block(qs):
        qa, qb = qs
        s1 = jnp.einsum('bqhd,bkhd->bhqk', qa.astype(F32), k1f) * scale
        s2 = jnp.einsum('bqhd,bkhd->bhqk', qb.astype(F32), k2f) * scale
        p = jax.nn.softmax(s1, axis=-1) - lam * jax.nn.softmax(s2, axis=-1)
        return jnp.einsum('bhqk,bkhe->bqhe', p, vf)

    o = lax.map(one_block, (blocks(q1), blocks(q2)))
    return jnp.moveaxis(o, 0, 1).reshape(B, T, H, 2 * Dh).astype(v.dtype)


def wkv_scan(r, decay, k, v, kk, a, s0, reverse):
    def step(S, inp):
        r_t, w_t, k_t, v_t, kk_t, a_t = inp
        sa = jnp.einsum('bhvk,bhk->bhv', S, -kk_t)
        S = S * w_t[:, :, None, :] + sa[..., None] * (kk_t * a_t)[:, :, None, :] + v_t[..., None] * k_t[:, :, None, :]
        return S, jnp.einsum('bhvk,bhk->bhv', S, r_t)

    xs = tuple(jnp.moveaxis(t.astype(F32), 1, 0) for t in (r, decay, k, v, kk, a))
    S, ys = lax.scan(step, s0.astype(F32), xs, reverse=reverse)
    return jnp.moveaxis(ys, 0, 1), S


def moe(h, router_w, router_b, w_gate, w_up, w_down):
    logits = jnp.einsum('btd,de->bte', h.astype(F32), router_w.astype(F32))
    aff = jax.nn.sigmoid(logits)
    sel = aff + router_b.astype(F32)
    per_group = N_EXPERTS // N_GROUPS
    grp = sel.reshape(sel.shape[:-1] + (N_GROUPS, per_group))
    grp_score = jnp.sum(lax.top_k(grp, TOP_K)[0], axis=-1)
    best = jnp.argmax(grp_score, axis=-1)
    in_group = (jnp.arange(N_EXPERTS) // per_group) == best[..., None]
    _, idx = lax.top_k(jnp.where(in_group, sel, -jnp.inf), TOP_K)
    wts = jnp.take_along_axis(aff, idx, axis=-1)
    wts = wts / jnp.sum(wts, axis=-1, keepdims=True)
    combine = jnp.sum(jax.nn.one_hot(idx, N_EXPERTS, dtype=F32) * wts[..., None], axis=-2)
    hg = jnp.einsum('btd,edf->btef', h, w_gate)
    hu = jnp.einsum('btd,edf->btef', h, w_up)
    act = jax.nn.silu(hg) * hu * combine[..., None].astype(h.dtype)
    return jnp.einsum('btef,efd->btd', act, w_down)


def token_mixers(h, lp, l, rope, ctx_k, ctx_v, ctx_state):
    B, T, _ = h.shape
    H, N = RWKV_HEADS, RWKV_HEAD
    proj = h @ lp['w_in']
    rkv, wd, ad, gd, aq, ak, av = jnp.split(proj, _split_points(), axis=-1)
    rkv = centred_shift(rkv, lp['shift_w'])
    r, k, v = (t.reshape(B, T, H, N) for t in jnp.split(rkv, 3, axis=-1))
    kk = (k * lp['k_k'].reshape(H, N)).astype(F32)
    kk = kk * lax.rsqrt(jnp.sum(kk * kk, axis=-1, keepdims=True) + 1e-12)
    k_a = lp['k_a'].reshape(H, N).astype(F32)
    r_k = lp['r_k'].astype(F32)
    wd = wd.reshape(B, T, 2, DECAY_LORA)
    ad = ad.reshape(B, T, 2, AAA_LORA)
    ys, bonuses, finals = [], [], []
    for d in range(2):
        w_pre = (lp['w0'][d] + jnp.tanh(wd[:, :, d]) @ lp['w_up'][d]).astype(F32)
        decay = jnp.exp(-jnp.exp(-jax.nn.softplus(-w_pre) - 0.5)).reshape(B, T, H, N)
        a = jax.nn.sigmoid((lp['a0'][d] + ad[:, :, d] @ lp['a_up'][d]).astype(F32)).reshape(B, T, H, N)
        k_d = k.astype(F32) * (1.0 + (a - 1.0) * k_a)
        s0 = jnp.zeros((B, H, N, N), F32) if ctx_state is None else ctx_state[:, d]
        y_d, s_d = wkv_scan(r, decay, k_d, v, kk, a, s0, reverse=(d == 1))
        ys.append(y_d)
        bonuses.append(jnp.sum(r.astype(F32) * k_d * r_k, axis=-1, keepdims=True) * v.astype(F32))
        finals.append(s_d)
    g = (jax.nn.sigmoid(gd) @ lp['g_up']).astype(F32)
    y_rwkv = (head_group_norm(ys[0] + ys[1], lp['ln_x_w'], lp['ln_x_b'])
              + (bonuses[0] + bonuses[1]).reshape(B, T, RWKV_W)) * g
    y_rwkv = y_rwkv.astype(h.dtype)
    HA, DH = ATT_HEADS, ATT_HEAD
    q = aq.reshape(B, T, HA, 2, DH)
    kq = ak.reshape(B, T, HA, 2, DH)
    q1, q2, k1, k2 = q[..., 0, :], q[..., 1, :], kq[..., 0, :], kq[..., 1, :]
    if rope is not None:
        cos, sin = rope
        q1, q2, k1, k2 = (apply_axial_rope(t, cos, sin) for t in (q1, q2, k1, k2))
    k_own = jnp.concatenate([k1, k2], axis=-1)
    v_own = av.reshape(B, T, HA, 2 * DH)
    if ctx_k is None:
        k_all, v_all = k_own, v_own
    else:
        k_all = jnp.concatenate([k_own, ctx_k.astype(k_own.dtype)], axis=1)
        v_all = jnp.concatenate([v_own, ctx_v.astype(v_own.dtype)], axis=1)
    lam_init = 0.8 - 0.6 * math.exp(-0.3 * l)
    lam = (jnp.exp(jnp.sum(lp['lambda_q1'].astype(F32) * lp['lambda_k1'].astype(F32)))
           - jnp.exp(jnp.sum(lp['lambda_q2'].astype(F32) * lp['lambda_k2'].astype(F32))) + lam_init)
    o = diff_attention(q1, q2, k_all[..., :DH], k_all[..., DH:], v_all, lam)
    o = rms_norm(o, lp['subln']) * (1.0 - lam_init)
    mix = jnp.concatenate([y_rwkv, o.reshape(B, T, ATT_W).astype(h.dtype)], axis=-1)
    return mix @ lp['w_out'], k_own, v_own, jnp.stack(finals, axis=1)


def trunk_layer(x, mod, lp, l, router_w, router_b, rope, ctx_k, ctx_v, ctx_state):
    sh1, sc1, g1, sh2, sc2, g2 = jnp.split(mod, 6, axis=-1)
    h = rms_norm(x, lp['norm1']) * (1.0 + sc1) + sh1
    mix, k_own, v_own, st = token_mixers(h, lp, l, rope, ctx_k, ctx_v, ctx_state)
    x = x + g1 * mix
    h = rms_norm(x, lp['norm2']) * (1.0 + sc2) + sh2
    x = x + g2 * moe(h, router_w, router_b, lp['exp_gate'], lp['exp_up'], lp['exp_down'])
    return x, k_own, v_own, st


def setup_inputs(seed: int = 0):
    key = jax.random.key(seed)
    ks = iter(jax.random.split(key, 48))

    def nrm(shape, scale):
        return jax.random.normal(next(ks), shape, F32) * scale

    D = D_MODEL
    E = N_EXPERTS
    return {
        'x_prompt': nrm((BATCH, SEQ, D), 1.0),
        'x_sample': nrm((DEC_BATCH, DEC_SEQ, D), 1.0),
        'cache_attn_k': nrm((DEC_BATCH, DEPTH, PAST_LEN, ATT_HEADS, 2 * ATT_HEAD), 1.0),
        'cache_attn_v': nrm((DEC_BATCH, DEPTH, PAST_LEN, ATT_HEADS, 2 * ATT_HEAD), 1.0),
        'state_rwkv': nrm((DEC_BATCH, DEPTH, 2, RWKV_HEADS, RWKV_HEAD, RWKV_HEAD), 0.3),
        'c': nrm((DEC_BATCH, D), 1.0),
        'c_ctx': nrm((D,), 1.0),
        'w_mod': nrm((DEPTH, D, 6 * D), 0.5 * D ** -0.5),
        'b_mod': nrm((DEPTH, 6 * D), 0.02),
        'norm1': 1.0 + nrm((DEPTH, D), 0.05),
        'w_in': nrm((DEPTH, D, IN_W), D ** -0.5),
        'shift_w': jnp.array([0.25, 0.5, 0.25], F32)[None, :, None] + nrm((DEPTH, 3, 3 * RWKV_W), 0.05),
        'k_k': 0.85 + nrm((DEPTH, RWKV_W), 0.05),
        'k_a': 1.0 + nrm((DEPTH, RWKV_W), 0.05),
        'r_k': nrm((DEPTH, RWKV_HEADS, RWKV_HEAD), 0.1),
        'w0': nrm((DEPTH, 2, RWKV_W), 0.5),
        'w_up': nrm((DEPTH, 2, DECAY_LORA, RWKV_W), 0.5 * DECAY_LORA ** -0.5),
        'a0': nrm((DEPTH, 2, RWKV_W), 0.5),
        'a_up': nrm((DEPTH, 2, AAA_LORA, RWKV_W), 0.5 * AAA_LORA ** -0.5),
        'g_up': nrm((DEPTH, GATE_LORA, RWKV_W), GATE_LORA ** -0.5),
        'ln_x_w': 1.0 + nrm((DEPTH, RWKV_W), 0.05),
        'ln_x_b': nrm((DEPTH, RWKV_W), 0.02),
        'lambda_q1': nrm((DEPTH, ATT_HEAD), 0.1),
        'lambda_k1': nrm((DEPTH, ATT_HEAD), 0.1),
        'lambda_q2': nrm((DEPTH, ATT_HEAD), 0.1),
        'lambda_k2': nrm((DEPTH, ATT_HEAD), 0.1),
        'subln': 1.0 + nrm((DEPTH, 2 * ATT_HEAD), 0.05),
        'w_out': nrm((DEPTH, MIX_W, D), MIX_W ** -0.5),
        'norm2': 1.0 + nrm((DEPTH, D), 0.05),
        'router_w': nrm((D, E), D ** -0.5),
        'router_b': nrm((E,), 0.01),
        'exp_gate': nrm((DEPTH, E, D, EXPERT_FF), D ** -0.5),
        'exp_up': nrm((DEPTH, E, D, EXPERT_FF), D ** -0.5),
        'exp_down': nrm((DEPTH, E, EXPERT_FF, D), EXPERT_FF ** -0.5),
        'norm_f': 1.0 + nrm((D,), 0.05),
    }


def reference(x_prompt, x_sample, cache_attn_k, cache_attn_v, state_rwkv, c, c_ctx,
              w_mod, b_mod, norm1, w_in, shift_w, k_k, k_a, r_k, w0, w_up, a0, a_up, g_up,
              ln_x_w, ln_x_b, lambda_q1, lambda_k1, lambda_q2, lambda_k2, subln, w_out, norm2,
              router_w, router_b, exp_gate, exp_up, exp_down, norm_f):
    rope = axial_rope(x_sample.shape[1])
    xp, xs = x_prompt, x_sample
    new_k, new_v, new_s = [], [], []
    for l in range(DEPTH):
        lp = dict(norm1=norm1[l], w_in=w_in[l], shift_w=shift_w[l], k_k=k_k[l], k_a=k_a[l],
                  r_k=r_k[l], w0=w0[l], w_up=w_up[l], a0=a0[l], a_up=a_up[l], g_up=g_up[l],
                  ln_x_w=ln_x_w[l], ln_x_b=ln_x_b[l], lambda_q1=lambda_q1[l],
                  lambda_k1=lambda_k1[l], lambda_q2=lambda_q2[l], lambda_k2=lambda_k2[l],
                  subln=subln[l], w_out=w_out[l], norm2=norm2[l], exp_gate=exp_gate[l],
                  exp_up=exp_up[l], exp_down=exp_down[l])
        mod_ctx = jax.nn.silu(c_ctx) @ w_mod[l] + b_mod[l]
        xp, k_own, v_own, st = trunk_layer(xp, mod_ctx, lp, l, router_w, router_b,
                                           None, None, None, None)
        new_k.append(k_own)
        new_v.append(v_own)
        new_s.append(st)
        mod_lat = (jax.nn.silu(c) @ w_mod[l] + b_mod[l])[:, None, :]
        xs, _, _, _ = trunk_layer(xs, mod_lat, lp, l, router_w, router_b, rope,
                                  cache_attn_k[:, l], cache_attn_v[:, l], state_rwkv[:, l])
    y_prompt = rms_norm(xp, norm_f)
    y_sample = rms_norm(xs, norm_f)
    new_cache_attn_k = jnp.stack(new_k, axis=1)
    new_cache_attn_v = jnp.stack(new_v, axis=1)
    new_state_rwkv = jnp.stack(new_s, axis=1)
    return (y_prompt, y_sample, new_cache_attn_k, new_cache_attn_v, new_state_rwkv)
```

```python
import functools
import math

import jax
import jax.numpy as jnp
from jax import lax
from jax.experimental import pallas as pl
from jax.experimental.pallas import tpu as pltpu

F32 = jnp.float32
MXU_DTYPE = jnp.bfloat16
HI = lax.Precision.HIGHEST

GRID_W = 64
RWKV_HEAD = 64
DECAY_LORA = 64
AAA_LORA = 64
GATE_LORA = 160
ATT_HEAD = 64
ROPE_AXIS = ATT_HEAD // 2
ROPE_THETA = 10000.0
N_GROUPS = 4
TOP_K = 2
NORM_EPS = 1e-6
GN_EPS = 64e-5

CHUNK = 64
TM = 512
LANE = 128
VMEM_LIMIT = 56 * 1024 * 1024


def _cparams(*sem):
    return pltpu.CompilerParams(dimension_semantics=sem, vmem_limit_bytes=VMEM_LIMIT)


def _sigmoid(x):
    return 1.0 / (1.0 + jnp.exp(-x))


def _dot(a, b):
    return jnp.dot(a, b, preferred_element_type=F32)


def _dot_hi(a, b):
    return jnp.dot(a, b, precision=HI, preferred_element_type=F32)


def _dot_nt(a, b, precision=None):
    return lax.dot_general(a, b, (((1,), (1,)), ((), ())), precision=precision,
                           preferred_element_type=F32)


def _dot_exact_rhs(x, m_bf16, passes):
    acc = None
    rem = x
    for _ in range(passes):
        piece = rem.astype(jnp.bfloat16)
        term = jnp.dot(piece, m_bf16, preferred_element_type=F32)
        acc = term if acc is None else acc + term
        rem = rem - piece.astype(F32)
    return acc


def _dot_exact_lhs(m_bf16, x, passes):
    acc = None
    rem = x
    for _ in range(passes):
        piece = rem.astype(jnp.bfloat16)
        term = jnp.dot(m_bf16, piece, preferred_element_type=F32)
        acc = term if acc is None else acc + term
        rem = rem - piece.astype(F32)
    return acc


def _head_sum_matrix(width, head):
    r = lax.broadcasted_iota(jnp.int32, (width, width), 0) // head
    c = lax.broadcasted_iota(jnp.int32, (width, width), 1) // head
    return jnp.where(r == c, 1.0, 0.0).astype(jnp.bfloat16)


def _mod_kernel(c_ref, w_ref, b_ref, o_ref):
    c = c_ref[...]
    s = c * _sigmoid(c)
    o_ref[0] = _dot_hi(s, w_ref[0]) + b_ref[0]


def _modulation(cvec, w_mod, b_mod):
    depth, d, six_d = w_mod.shape
    tn = six_d // 4
    return pl.pallas_call(
        _mod_kernel,
        grid=(depth, six_d // tn),
        in_specs=[pl.BlockSpec((8, d), lambda l, j: (0, 0)),
                  pl.BlockSpec((1, d, tn), lambda l, j: (l, 0, j)),
                  pl.BlockSpec((1, 1, tn), lambda l, j: (l, 0, j))],
        out_specs=pl.BlockSpec((1, 8, tn), lambda l, j: (l, 0, j)),
        out_shape=jax.ShapeDtypeStruct((depth, 8, six_d), F32),
        compiler_params=_cparams("parallel", "parallel"),
        name="modulation",
    )(cvec, w_mod, b_mod.reshape(depth, 1, six_d))


def _in_proj_kernel(x_ref, mod_ref, g_ref, w_ref, rkv_ref, qkv_ref, lora_ref, *, d, rw):
    x = x_ref[...]
    ms = jnp.mean(x * x, axis=-1, keepdims=True)
    y = x * lax.rsqrt(ms + NORM_EPS) * g_ref[...]
    m = mod_ref[0]
    h = y * (1.0 + m[:, d:2 * d]) + m[:, 0:d]
    p = _dot(h.astype(MXU_DTYPE), w_ref[...])
    rkv_ref[...] = p[:, :3 * rw]
    qkv_ref[...] = p[:, 3 * rw:6 * rw]
    lora_ref[...] = p[:, 6 * rw:]


def _in_proj(x, mod_l, norm_g, w_in_r, geom):
    n, d = x.shape
    rw = geom["rw"]
    lw = w_in_r.shape[1] - 6 * rw
    row_of = geom["mod_row_of_tile"]
    return pl.pallas_call(
        functools.partial(_in_proj_kernel, d=d, rw=rw),
        grid=(n // TM,),
        in_specs=[pl.BlockSpec((TM, d), lambda i: (i, 0)),
                  pl.BlockSpec((1, 1, 6 * d), lambda i: (row_of(i), 0, 0)),
                  pl.BlockSpec((1, d), lambda i: (0, 0)),
                  pl.BlockSpec(w_in_r.shape, lambda i: (0, 0))],
        out_specs=[pl.BlockSpec((TM, 3 * rw), lambda i: (i, 0)),
                   pl.BlockSpec((TM, 3 * rw), lambda i: (i, 0)),
                   pl.BlockSpec((TM, lw), lambda i: (i, 0))],
        out_shape=[jax.ShapeDtypeStruct((n, 3 * rw), F32),
                   jax.ShapeDtypeStruct((n, 3 * rw), F32),
                   jax.ShapeDtypeStruct((n, lw), F32)],
        compiler_params=_cparams("parallel"),
        name="in_proj",
    )(x, mod_l, norm_g, w_in_r)


def _rwkv_prep_kernel(rkv_ref, prev_ref, next_ref, lora_ref, shw_ref, kk_ref, ka_ref, rk_ref,
                      w0_ref, wup_ref, a0_ref, aup_ref, gup_ref,
                      at_ref, bt_ref, kt_ref, rt_ref, v_ref, pc_ref, bonus_ref, g_ref,
                      cl_ref, *, rw, nctx, seq, dec_seq):
    i = pl.program_id(0)
    x = rkv_ref[...]
    row = i * TM + lax.broadcasted_iota(jnp.int32, (TM, 1), 0)
    is_ctx = row < nctx
    pos_c = row % seq
    pos_l = (row - nctx) % dec_seq
    pos = jnp.where(is_ctx, pos_c, pos_l)
    first = pos == 0
    last = pos == jnp.where(is_ctx, seq - 1, dec_seq - 1)
    loc = lax.broadcasted_iota(jnp.int32, (TM, 1), 0)
    xp = pltpu.roll(x, 1, 0)
    xp = jnp.where(loc == 0, prev_ref[7:8, :], xp)
    xp = jnp.where(first, 0.0, xp)
    xn = pltpu.roll(x, TM - 1, 0)
    xn = jnp.where(loc == TM - 1, next_ref[0:1, :], xn)
    xn = jnp.where(last, 0.0, xn)
    xs = shw_ref[0:1, :] * xp + shw_ref[1:2, :] * x + shw_ref[2:3, :] * xn
    r = xs[:, :rw]
    k = xs[:, rw:2 * rw]
    v = xs[:, 2 * rw:]
    hs = _head_sum_matrix(rw, RWKV_HEAD)

    kk = k * kk_ref[...]
    kk = kk * lax.rsqrt(_dot_exact_rhs(kk * kk, hs, 3) + 1e-12)

    lora = lora_ref[...]
    gd = lora[:, 2 * DECAY_LORA + 2 * AAA_LORA:]
    g_ref[...] = _dot(_sigmoid(gd).astype(MXU_DTYPE), gup_ref[...])
    v_ref[...] = v

    ti = lax.broadcasted_iota(jnp.int32, (CHUNK, CHUNK), 0)
    tj = lax.broadcasted_iota(jnp.int32, (CHUNK, CHUNK), 1)
    nchunk = TM // CHUNK
    ci = lax.broadcasted_iota(jnp.int32, (nchunk, TM), 0)
    cj = lax.broadcasted_iota(jnp.int32, (nchunk, TM), 1) // CHUNK
    chunk_sel = jnp.where(ci == cj, 1.0, 0.0).astype(jnp.bfloat16)

    bonus = jnp.zeros((TM, rw), F32)
    for dr in range(2):
        wd = lora[:, dr * DECAY_LORA:(dr + 1) * DECAY_LORA]
        ad = lora[:, 2 * DECAY_LORA + dr * AAA_LORA:2 * DECAY_LORA + (dr + 1) * AAA_LORA]
        w_pre = w0_ref[dr:dr + 1, :] + _dot_hi(jnp.tanh(wd), wup_ref[dr])
        z = -w_pre
        softplus = jnp.maximum(z, 0.0) + jnp.log(1.0 + jnp.exp(-jnp.abs(z)))
        logw = -jnp.exp(-softplus - 0.5)
        a = _sigmoid(a0_ref[dr:dr + 1, :] + _dot_hi(ad, aup_ref[dr]))
        k_d = k * (1.0 + (a - 1.0) * ka_ref[...])
        bonus = bonus + _dot_exact_rhs(r * k_d * rk_ref[...], hs, 3) * v
        tri = jnp.where(tj <= ti if dr == 0 else tj >= ti, 1.0, 0.0).astype(jnp.bfloat16)
        for c in range(nchunk):
            sl = pl.ds(c * CHUNK, CHUNK)
            cl_ref[sl, :] = _dot_exact_lhs(tri, logw[c * CHUNK:(c + 1) * CHUNK, :], 3)
        cl = cl_ref[...]
        e_pos = jnp.exp(cl)
        e_neg = jnp.exp(-cl)
        at_ref[dr] = jnp.exp(cl - logw) * (-kk)
        bt_ref[dr] = kk * a * e_neg
        kt_ref[dr] = k_d * e_neg
        rt_ref[dr] = r * e_pos
        pc_ref[dr] = jnp.exp(_dot_exact_lhs(chunk_sel, logw, 3))
    bonus_ref[...] = bonus


def _rwkv_prep(rkv, lora, lp, geom):
    n, w3 = rkv.shape
    rw = geom["rw"]
    lw = lora.shape[1]
    nb8 = n // 8
    t8 = TM // 8
    tok = lambda i: (i, 0)
    const2 = lambda i: (0, 0)
    const3 = lambda i: (0, 0, 0)
    dir_tok = lambda i: (0, i, 0)
    kern = functools.partial(_rwkv_prep_kernel, rw=rw, nctx=geom["nctx"], seq=geom["seq"],
                             dec_seq=geom["dec_seq"])
    outs = pl.pallas_call(
        kern,
        grid=(n // TM,),
        in_specs=[pl.BlockSpec((TM, w3), tok),
                  pl.BlockSpec((8, w3), lambda i: (jnp.maximum(i * t8 - 1, 0), 0)),
                  pl.BlockSpec((8, w3), lambda i: (jnp.minimum((i + 1) * t8, nb8 - 1), 0)),
                  pl.BlockSpec((TM, lw), tok),
                  pl.BlockSpec((3, w3), const2),
                  pl.BlockSpec((1, rw), const2),
                  pl.BlockSpec((1, rw), const2),
                  pl.BlockSpec((1, rw), const2),
                  pl.BlockSpec((2, rw), const2),
                  pl.BlockSpec((2, DECAY_LORA, rw), const3),
                  pl.BlockSpec((2, rw), const2),
                  pl.BlockSpec((2, AAA_LORA, rw), const3),
                  pl.BlockSpec(lp["g_up"].shape, const2)],
        out_specs=[pl.BlockSpec((2, TM, rw), dir_tok)] * 4
                  + [pl.BlockSpec((TM, rw), tok),
                     pl.BlockSpec((2, TM // CHUNK, rw), dir_tok),
                     pl.BlockSpec((TM, rw), tok),
                     pl.BlockSpec((TM, rw), tok)],
        out_shape=[jax.ShapeDtypeStruct((2, n, rw), F32)] * 4
                  + [jax.ShapeDtypeStruct((n, rw), F32),
                     jax.ShapeDtypeStruct((2, n // CHUNK, rw), F32),
                     jax.ShapeDtypeStruct((n, rw), F32),
                     jax.ShapeDtypeStruct((n, rw), F32)],
        scratch_shapes=[pltpu.VMEM((TM, rw), F32)],
        compiler_params=_cparams("parallel"),
        name="rwkv_prep",
    )(rkv, rkv, rkv, lora, lp["shift_w"], lp["k_k"], lp["k_a"], lp["r_k"], lp["w0"], lp["w_up"],
      lp["a0"], lp["a_up"], lp["g_up"])
    return outs


def _tri_masks(reverse):
    ti = lax.broadcasted_iota(jnp.int32, (CHUNK, CHUNK), 0)
    tj = lax.broadcasted_iota(jnp.int32, (CHUNK, CHUNK), 1)
    if reverse:
        return tj > ti, tj >= ti
    return tj < ti, tj <= ti


def _lmat_kernel(at_ref, bt_ref, l_ref, *, rw):
    nchunk = TM // CHUNK
    heads = rw // RWKV_HEAD
    for dr in range(2):
        strict, _ = _tri_masks(dr == 1)
        for c in range(nchunk):
            rs = slice(c * CHUNK, (c + 1) * CHUNK)
            for h in range(heads):
                ls = slice(h * RWKV_HEAD, (h + 1) * RWKV_HEAD)
                m = _dot_nt(at_ref[dr, rs, ls], bt_ref[dr, rs, ls], HI)
                l_ref[dr, rs, ls] = jnp.where(strict, m, 0.0)


def _lmat(at, bt, geom):
    _, n, rw = at.shape
    spec = pl.BlockSpec((2, TM, rw), lambda i: (0, i, 0))
    return pl.pallas_call(
        functools.partial(_lmat_kernel, rw=rw),
        grid=(n // TM,),
        in_specs=[spec, spec],
        out_specs=spec,
        out_shape=jax.ShapeDtypeStruct((2, n, rw), F32),
        compiler_params=_cparams("parallel"),
        name="scan_lmat",
    )(at, bt)


def _tri_inv_kernel(l_ref, t_ref):
    jj = lax.broadcasted_iota(jnp.int32, (CHUNK, LANE), 0)

    def row(t, carry):
        def col(i, acc):
            return acc + l_ref[t, pl.ds(i, 1), :] * t_ref[i]
        acc = lax.fori_loop(0, t, col, jnp.where(jj == t, 1.0, 0.0))
        t_ref[t] = acc
        return carry

    lax.fori_loop(0, CHUNK, row, 0)


def _tri_inv(lt):
    nsys = lt.shape[2]
    spec = pl.BlockSpec((CHUNK, CHUNK, LANE), lambda i: (0, 0, i))
    return pl.pallas_call(
        _tri_inv_kernel,
        grid=(nsys // LANE,),
        in_specs=[spec],
        out_specs=spec,
        out_shape=jax.ShapeDtypeStruct(lt.shape, F32),
        compiler_params=_cparams("parallel"),
        name="scan_tri_inv",
    )(lt)


def _scan_kernel(at_ref, bt_ref, kt_ref, rt_ref, tm_ref, v_ref, pc_ref, s0_ref,
                 y_ref, fin_ref, s_ref, vt_ref, yt_ref, *, rw, reverse, geom):
    j = pl.program_id(0)
    ntile = pl.num_programs(0)
    tile = (ntile - 1 - j) if reverse else j
    nchunk = TM // CHUNK
    heads = rw // RWKV_HEAD
    strict, incl = _tri_masks(reverse)
    cchunks = geom["nctx"] // CHUNK
    seq_chunks = geom["seq"] // CHUNK
    dec_chunks = geom["dec_seq"] // CHUNK

    vt_ref[...] = v_ref[...].T
    order = range(nchunk - 1, -1, -1) if reverse else range(nchunk)
    for c in order:
        gch = tile * nchunk + c
        in_ctx = gch < cchunks
        lat = gch - cchunks
        seq_id = jnp.where(in_ctx, gch // seq_chunks, geom["batch"] + lat // dec_chunks)
        local = jnp.where(in_ctx, gch % seq_chunks, lat % dec_chunks)
        length = jnp.where(in_ctx, seq_chunks, dec_chunks)
        starts = (local == length - 1) if reverse else (local == 0)
        ends = (local == 0) if reverse else (local == length - 1)

        @pl.when(starts)
        def _():
            s_ref[...] = s0_ref[seq_id]

        rs = slice(c * CHUNK, (c + 1) * CHUNK)
        for h in range(heads):
            ls = slice(h * RWKV_HEAD, (h + 1) * RWKV_HEAD)
            a = at_ref[rs, ls]
            b = bt_ref[rs, ls]
            k = kt_ref[rs, ls]
            r = rt_ref[rs, ls]
            tmat = tm_ref[rs, ls]
            vt = vt_ref[ls, rs]
            s = s_ref[:, ls]
            aak = jnp.where(strict, _dot_nt(a, k, HI), 0.0)
            arb = jnp.where(incl, _dot_nt(r, b, HI), 0.0)
            ark = jnp.where(incl, _dot_nt(r, k, HI), 0.0)
            x2t = _dot_nt(vt, aak, HI)
            w = _dot_hi(tmat, a)
            ut = _dot_nt(s, w, HI) + _dot_nt(x2t, tmat, HI)
            yt = _dot_nt(s, r, HI) + _dot_nt(ut, arb, HI) + _dot_nt(vt, ark, HI)
            yt_ref[ls, rs] = yt
            s_new = (s + _dot_hi(ut, b) + _dot_hi(vt, k)) * pc_ref[c:c + 1, ls]
            s_ref[:, ls] = s_new

        @pl.when(ends)
        def _():
            fin_ref[seq_id] = s_ref[...]

    y_ref[...] = yt_ref[...].T


def _scan(at, bt, kt, rt, tmat, v, pc, s0, dr, geom):
    n, rw = v.shape
    ntile = n // TM
    reverse = dr == 1
    nseq = s0.shape[0]
    if reverse:
        tix = lambda j: (dr, ntile - 1 - j, 0)
        vix = lambda j: (ntile - 1 - j, 0)
    else:
        tix = lambda j: (dr, j, 0)
        vix = lambda j: (j, 0)
    dspec = pl.BlockSpec((None, TM, rw), tix)
    kern = functools.partial(_scan_kernel, rw=rw, reverse=reverse, geom=geom)
    return pl.pallas_call(
        kern,
        grid=(ntile,),
        in_specs=[dspec, dspec, dspec, dspec, dspec,
                  pl.BlockSpec((TM, rw), vix),
                  pl.BlockSpec((None, TM // CHUNK, rw), tix),
                  pl.BlockSpec(s0.shape, lambda j: (0, 0, 0))],
        out_specs=[pl.BlockSpec((TM, rw), vix),
                   pl.BlockSpec(s0.shape, lambda j: (0, 0, 0))],
        out_shape=[jax.ShapeDtypeStruct((n, rw), F32),
                   jax.ShapeDtypeStruct(s0.shape, F32)],
        scratch_shapes=[pltpu.VMEM((RWKV_HEAD, rw), F32),
                        pltpu.VMEM((rw, TM), F32),
                        pltpu.VMEM((rw, TM), F32)],
        compiler_params=_cparams("arbitrary"),
        name="scan_seq_rev" if reverse else "scan_seq_fwd",
    )(at, bt, kt, rt, tmat, v, pc, s0)


def _rope(x, cos, sin_signed):
    lane = lax.broadcasted_iota(jnp.int32, x.shape, 1)
    half = ROPE_AXIS // 2
    swapped = jnp.where(lane % ROPE_AXIS < half, pltpu.roll(x, LANE - half, 1), pltpu.roll(x, half, 1))
    return x * cos + swapped * sin_signed


def _attn_kernel(*refs, t_own, past, use_rope, lam_init, tq):
    if use_rope:
        (q_ref, k_ref, v_ref, ck_ref, cv_ref, cosq_ref, sinq_ref, cosk_ref, sink_ref,
         lq1_ref, lk1_ref, lq2_ref, lk2_ref, sub_ref, o_ref, kall_ref, vall_ref) = refs
    else:
        (q_ref, k_ref, v_ref, lq1_ref, lk1_ref, lq2_ref, lk2_ref, sub_ref,
         o_ref, kall_ref, vall_ref) = refs
    i = pl.program_id(2)

    @pl.when(i == 0)
    def _():
        k = k_ref[...]
        if use_rope:
            k = _rope(k, cosk_ref[...], sink_ref[...])
        kall_ref[0:t_own, :] = k.astype(MXU_DTYPE)
        vall_ref[0:t_own, :] = v_ref[...].astype(MXU_DTYPE)
        if past:
            kall_ref[t_own:t_own + past, :] = ck_ref[...].astype(MXU_DTYPE)
            vall_ref[t_own:t_own + past, :] = cv_ref[...].astype(MXU_DTYPE)

    lam = (jnp.exp(jnp.sum(lq1_ref[...] * lk1_ref[...], axis=-1, keepdims=True))
           - jnp.exp(jnp.sum(lq2_ref[...] * lk2_ref[...], axis=-1, keepdims=True)) + lam_init)
    q = q_ref[...]
    if use_rope:
        q = _rope(q, cosq_ref[...], sinq_ref[...])
    q = q * (ATT_HEAD ** -0.5)
    lane = lax.broadcasted_iota(jnp.int32, q.shape, 1)
    q1 = jnp.where(lane < ATT_HEAD, q, 0.0).astype(MXU_DTYPE)
    q2 = jnp.where(lane >= ATT_HEAD, q, 0.0).astype(MXU_DTYPE)
    kall = kall_ref[...]
    s1 = _dot_nt(q1, kall)
    s2 = _dot_nt(q2, kall)
    e1 = jnp.exp(s1 - jnp.max(s1, axis=-1, keepdims=True))
    e2 = jnp.exp(s2 - jnp.max(s2, axis=-1, keepdims=True))
    inv1 = 1.0 / jnp.sum(e1, axis=-1, keepdims=True)
    inv2 = lam / jnp.sum(e2, axis=-1, keepdims=True)
    p = e1 * inv1 - e2 * inv2
    o = _dot(p.astype(MXU_DTYPE), vall_ref[...])
    o = o * lax.rsqrt(jnp.mean(o * o, axis=-1, keepdims=True) + NORM_EPS) * sub_ref[...]
    o_ref[...] = o * (1.0 - lam_init)


def _attention(qkv, row_off, nbatch, t_own, ctx_k, ctx_v, rope_tabs, lp, lam_init, geom):
    n, w3 = qkv.shape
    aw = w3 // 3
    hd2 = 2 * ATT_HEAD
    nh = aw // hd2
    tq = min(256, t_own)
    nq = t_own // tq
    past = 0 if ctx_k is None else ctx_k.shape[1]
    use_rope = rope_tabs is not None
    qoff = row_off // tq
    koff = row_off // t_own
    in_specs = [pl.BlockSpec((tq, hd2), lambda b, h, i: (qoff + b * nq + i, h)),
                pl.BlockSpec((t_own, hd2), lambda b, h, i: (koff + b, nh + h)),
                pl.BlockSpec((t_own, hd2), lambda b, h, i: (koff + b, 2 * nh + h))]
    args = [qkv, qkv, qkv]
    if use_rope:
        cos_t, sin_t = rope_tabs
        in_specs += [pl.BlockSpec((None, past, hd2), lambda b, h, i: (b, 0, h)),
                     pl.BlockSpec((None, past, hd2), lambda b, h, i: (b, 0, h)),
                     pl.BlockSpec((tq, hd2), lambda b, h, i: (i, 0)),
                     pl.BlockSpec((tq, hd2), lambda b, h, i: (i, 0)),
                     pl.BlockSpec((t_own, hd2), lambda b, h, i: (0, 0)),
                     pl.BlockSpec((t_own, hd2), lambda b, h, i: (0, 0))]
        args += [ctx_k, ctx_v, cos_t, sin_t, cos_t, sin_t]
    small = pl.BlockSpec((1, ATT_HEAD), lambda b, h, i: (0, 0))
    in_specs += [small, small, small, small, pl.BlockSpec((1, hd2), lambda b, h, i: (0, 0))]
    args += [lp["lambda_q1"], lp["lambda_k1"], lp["lambda_q2"], lp["lambda_k2"], lp["subln"]]
    kern = functools.partial(_attn_kernel, t_own=t_own, past=past, use_rope=use_rope,
                             lam_init=lam_init, tq=tq)
    return pl.pallas_call(
        kern,
        grid=(nbatch, nh, nq),
        in_specs=in_specs,
        out_specs=pl.BlockSpec((tq, hd2), lambda b, h, i: (b * nq + i, h)),
        out_shape=jax.ShapeDtypeStruct((nbatch * t_own, aw), F32),
        scratch_shapes=[pltpu.VMEM((t_own + past, hd2), MXU_DTYPE),
                        pltpu.VMEM((t_own + past, hd2), MXU_DTYPE)],
        compiler_params=_cparams("parallel", "parallel", "arbitrary"),
        name="diff_attn_lat" if use_rope else "diff_attn_ctx",
    )(*args)


def _mix_out_kernel(y0_ref, y1_ref, bonus_ref, g_ref, o_ref, x_ref, mod_ref, lnw_ref, lnb_ref,
                    wout_ref, n2_ref, xo_ref, h2_ref, *, d, rw):
    y = y0_ref[...] + y1_ref[...]
    hs = _head_sum_matrix(rw, RWKV_HEAD)
    inv = 1.0 / RWKV_HEAD
    mu = _dot_exact_rhs(y, hs, 3) * inv
    yc = y - mu
    var = _dot_exact_rhs(yc * yc, hs, 3) * inv
    yn = yc * lax.rsqrt(var + GN_EPS)
    y_rwkv = (yn * lnw_ref[...] + lnb_ref[...] + bonus_ref[...]) * g_ref[...]
    mix = (_dot(y_rwkv.astype(MXU_DTYPE), wout_ref[0:rw, :])
           + _dot(o_ref[...].astype(MXU_DTYPE), wout_ref[rw:, :]))
    m = mod_ref[0]
    x = x_ref[...] + m[:, 2 * d:3 * d] * mix
    xo_ref[...] = x
    hn = x * lax.rsqrt(jnp.mean(x * x, axis=-1, keepdims=True) + NORM_EPS) * n2_ref[...]
    h2_ref[...] = hn * (1.0 + m[:, 4 * d:5 * d]) + m[:, 3 * d:4 * d]


def _mix_out(y0, y1, bonus, g, o_att, x, mod_l, lp, geom):
    n, d = x.shape
    rw = geom["rw"]
    aw = o_att.shape[1]
    row_of = geom["mod_row_of_tile"]
    tok = lambda i: (i, 0)
    const = lambda i: (0, 0)
    return pl.pallas_call(
        functools.partial(_mix_out_kernel, d=d, rw=rw),
        grid=(n // TM,),
        in_specs=[pl.BlockSpec((TM, rw), tok), pl.BlockSpec((TM, rw), tok),
                  pl.BlockSpec((TM, rw), tok), pl.BlockSpec((TM, rw), tok),
                  pl.BlockSpec((TM, aw), tok), pl.BlockSpec((TM, d), tok),
                  pl.BlockSpec((1, 1, 6 * d), lambda i: (row_of(i), 0, 0)),
                  pl.BlockSpec((1, rw), const), pl.BlockSpec((1, rw), const),
                  pl.BlockSpec((rw + aw, d), const), pl.BlockSpec((1, d), const)],
        out_specs=[pl.BlockSpec((TM, d), tok), pl.BlockSpec((TM, d), tok)],
        out_shape=[jax.ShapeDtypeStruct((n, d), F32), jax.ShapeDtypeStruct((n, d), F32)],
        compiler_params=_cparams("parallel"),
        name="mix_out",
    )(y0, y1, bonus, g, o_att, x, mod_l, lp["ln_x_w"], lp["ln_x_b"], lp["w_out"], lp["norm2"])


def _route(h2, rw_ref, rb_ref, n_exp):
    logits = _dot_hi(h2, rw_ref[...])
    aff = _sigmoid(logits)
    sel = aff + rb_ref[...]
    per_group = n_exp // N_GROUPS
    col = [sel[:, e:e + 1] for e in range(n_exp)]
    affc = [aff[:, e:e + 1] for e in range(n_exp)]
    scores = []
    for gi in range(N_GROUPS):
        a, b, c, dd = col[gi * per_group:(gi + 1) * per_group]
        hi1, lo1 = jnp.maximum(a, b), jnp.minimum(a, b)
        hi2, lo2 = jnp.maximum(c, dd), jnp.minimum(c, dd)
        scores.append(jnp.maximum(hi1, hi2) + jnp.maximum(jnp.minimum(hi1, hi2), jnp.maximum(lo1, lo2)))
    best = jnp.zeros_like(scores[0], dtype=jnp.int32)
    bs = scores[0]
    for gi in range(1, N_GROUPS):
        better = scores[gi] > bs
        best = jnp.where(better, gi, best)
        bs = jnp.where(better, scores[gi], bs)
    neg = jnp.full_like(col[0], -jnp.inf)
    masked = [jnp.where(best == (e // per_group), col[e], neg) for e in range(n_exp)]
    picks = []
    for _ in range(TOP_K):
        idx = jnp.zeros_like(best)
        val = masked[0]
        for e in range(1, n_exp):
            better = masked[e] > val
            idx = jnp.where(better, e, idx)
            val = jnp.where(better, masked[e], val)
        picks.append(idx)
        masked = [jnp.where(idx == e, neg, masked[e]) for e in range(n_exp)]
    wts = []
    for idx in picks:
        w = jnp.zeros_like(col[0])
        for e in range(n_exp):
            w = jnp.where(idx == e, affc[e], w)
        wts.append(w)
    total = wts[0] + wts[1]
    lane = lax.broadcasted_iota(jnp.int32, sel.shape, 1)
    combine = jnp.zeros_like(sel)
    for idx, w in zip(picks, wts):
        combine = combine + jnp.where(lane == idx, w / total, 0.0)
    return combine


def _moe_kernel(h2_ref, x_ref, mod_ref, rw_ref, rb_ref, wg_ref, wu_ref, wd_ref, nf_ref,
                xo_ref, yf_ref, comb_ref, acc_ref, hb_ref, *, d, n_exp):
    e = pl.program_id(1)

    @pl.when(e == 0)
    def _():
        h2 = h2_ref[...]
        comb_ref[...] = _route(h2, rw_ref, rb_ref, n_exp)
        hb_ref[...] = h2.astype(MXU_DTYPE)
        acc_ref[...] = jnp.zeros_like(acc_ref)

    hb = hb_ref[...]
    hg = _dot(hb, wg_ref[0])
    hu = _dot(hb, wu_ref[0])
    comb = comb_ref[...]
    lane = lax.broadcasted_iota(jnp.int32, comb.shape, 1)
    ce = jnp.sum(jnp.where(lane == e, comb, 0.0), axis=-1, keepdims=True)
    act = hg * _sigmoid(hg) * hu * ce
    acc_ref[...] += _dot(act.astype(MXU_DTYPE), wd_ref[0])

    @pl.when(e == n_exp - 1)
    def _():
        m = mod_ref[0]
        x = x_ref[...] + m[:, 5 * d:6 * d] * acc_ref[...]
        xo_ref[...] = x
        yf_ref[...] = x * lax.rsqrt(jnp.mean(x * x, axis=-1, keepdims=True) + NORM_EPS) * nf_ref[...]


def _moe(h2, x, mod_l, router_w_p, router_b_p, wg, wu, wd, norm_f, geom):
    n, d = x.shape
    n_exp, _, ff = wg.shape
    row_of = geom["mod_row_of_tile"]
    tok = lambda i, e: (i, 0)
    const = lambda i, e: (0, 0)
    return pl.pallas_call(
        functools.partial(_moe_kernel, d=d, n_exp=n_exp),
        grid=(n // TM, n_exp),
        in_specs=[pl.BlockSpec((TM, d), tok), pl.BlockSpec((TM, d), tok),
                  pl.BlockSpec((1, 1, 6 * d), lambda i, e: (row_of(i), 0, 0)),
                  pl.BlockSpec((d, LANE), const), pl.BlockSpec((1, LANE), const),
                  pl.BlockSpec((1, d, ff), lambda i, e: (e, 0, 0)),
                  pl.BlockSpec((1, d, ff), lambda i, e: (e, 0, 0)),
                  pl.BlockSpec((1, ff, d), lambda i, e: (e, 0, 0)),
                  pl.BlockSpec((1, d), const)],
        out_specs=[pl.BlockSpec((TM, d), tok), pl.BlockSpec((TM, d), tok)],
        out_shape=[jax.ShapeDtypeStruct((n, d), F32), jax.ShapeDtypeStruct((n, d), F32)],
        scratch_shapes=[pltpu.VMEM((TM, LANE), F32), pltpu.VMEM((TM, d), F32),
                        pltpu.VMEM((TM, d), MXU_DTYPE)],
        compiler_params=_cparams("parallel", "arbitrary"),
        name="moe",
    )(h2, x, mod_l, router_w_p, router_b_p, wg, wu, wd, norm_f)


def _rope_tables(t):
    rows = t // GRID_W
    row = jnp.repeat(jnp.arange(rows), GRID_W)
    col = jnp.tile(jnp.arange(GRID_W), rows)
    inv = 1.0 / (ROPE_THETA ** (jnp.arange(0, ROPE_AXIS, 2, dtype=F32) / ROPE_AXIS))
    ang = jnp.concatenate([row[:, None] * inv, col[:, None] * inv], axis=-1)
    half = ROPE_AXIS // 2
    ang64 = jnp.concatenate([ang[:, :half], ang[:, :half], ang[:, half:], ang[:, half:]], axis=-1)
    sign = jnp.tile(jnp.concatenate([-jnp.ones((half,), F32), jnp.ones((half,), F32)]), 2)
    cos64 = jnp.cos(ang64)
    sin64 = jnp.sin(ang64) * sign
    return jnp.tile(cos64, (1, 2)), jnp.tile(sin64, (1, 2))


def _to_systems(lmat, nchunks, heads):
    l5 = lmat.reshape(2, nchunks, CHUNK, heads, RWKV_HEAD)
    fwd = jnp.transpose(l5[0], (1, 3, 0, 2))
    bwd = jnp.transpose(l5[1], (3, 1, 0, 2))
    return jnp.stack([fwd, bwd], axis=2).reshape(CHUNK, CHUNK, 2 * nchunks * heads)


def _from_systems(tsys, nchunks, heads):
    t5 = tsys.reshape(CHUNK, CHUNK, 2, nchunks, heads)
    fwd = jnp.transpose(t5[:, :, 0], (2, 0, 3, 1))
    bwd = jnp.transpose(t5[:, :, 1], (2, 1, 3, 0))
    return jnp.stack([fwd, bwd], axis=0).reshape(2, nchunks * CHUNK, heads * RWKV_HEAD)


def kernel(x_prompt, x_sample, cache_attn_k, cache_attn_v, state_rwkv, c, c_ctx, w_mod, b_mod, norm1, w_in, shift_w, k_k, k_a, r_k, w0, w_up, a0, a_up, g_up, ln_x_w, ln_x_b, lambda_q1, lambda_k1, lambda_q2, lambda_k2, subln, w_out, norm2, router_w, router_b, exp_gate, exp_up, exp_down, norm_f):
    batch, seq, d = x_prompt.shape
    dec_batch, dec_seq, _ = x_sample.shape
    depth = w_mod.shape[0]
    past = cache_attn_k.shape[2]
    rw = k_k.shape[1]
    heads = rw // RWKV_HEAD
    aw = w_out.shape[1] - rw
    n_exp = router_w.shape[1]
    nctx = batch * seq
    nlat = dec_batch * dec_seq
    n = nctx + nlat
    assert nctx % TM == 0 and dec_seq % TM == 0 and seq % CHUNK == 0 and nctx % dec_seq == 0
    assert dec_batch + 1 <= 8 and n_exp <= LANE and n_exp // N_GROUPS == 4 and aw == rw
    nchunks = n // CHUNK
    nseq = batch + dec_batch
    tiles_ctx = nctx // TM
    tiles_seq = dec_seq // TM

    def mod_row_of_tile(i):
        return jnp.where(i < tiles_ctx, 0, 1 + (i - tiles_ctx) // tiles_seq)

    geom = dict(rw=rw, nctx=nctx, seq=seq, dec_seq=dec_seq, batch=batch,
                mod_row_of_tile=mod_row_of_tile)

    cvec = jnp.zeros((8, d), F32).at[0].set(c_ctx).at[1:1 + dec_batch].set(c)
    mod = _modulation(cvec, w_mod, b_mod).reshape(depth, 8, 1, 6 * d)

    o_rkv = 3 * rw
    o_lora = o_rkv
    n_lora = 2 * DECAY_LORA + 2 * AAA_LORA + GATE_LORA
    lora_w = -(-n_lora // LANE) * LANE
    gate_w = lora_w - 2 * DECAY_LORA - 2 * AAA_LORA
    w_in_r = jnp.concatenate(
        [w_in[:, :, :o_rkv], w_in[:, :, o_lora + n_lora:], w_in[:, :, o_lora:o_lora + n_lora],
         jnp.zeros((depth, d, lora_w - n_lora), F32)], axis=-1).astype(MXU_DTYPE)
    g_up_p = jnp.concatenate([g_up, jnp.zeros((depth, gate_w - GATE_LORA, rw), F32)],
                             axis=1).astype(MXU_DTYPE)
    w_out_c = w_out.astype(MXU_DTYPE)
    wg = exp_gate.astype(MXU_DTYPE)
    wu = exp_up.astype(MXU_DTYPE)
    wd = exp_down.astype(MXU_DTYPE)
    router_w_p = jnp.zeros((d, LANE), F32).at[:, :n_exp].set(router_w)
    router_b_p = jnp.zeros((1, LANE), F32).at[0, :n_exp].set(router_b)
    rope_tabs = _rope_tables(dec_seq)

    x = jnp.concatenate([x_prompt.reshape(nctx, d), x_sample.reshape(nlat, d)], axis=0)
    new_k, new_v, new_s = [], [], []
    y_final = None
    for l in range(depth):
        lp = dict(shift_w=shift_w[l], k_k=k_k[l][None], k_a=k_a[l][None], r_k=r_k[l].reshape(1, rw),
                  w0=w0[l], w_up=w_up[l], a0=a0[l], a_up=a_up[l], g_up=g_up_p[l],
                  ln_x_w=ln_x_w[l][None], ln_x_b=ln_x_b[l][None],
                  lambda_q1=lambda_q1[l][None], lambda_k1=lambda_k1[l][None],
                  lambda_q2=lambda_q2[l][None], lambda_k2=lambda_k2[l][None],
                  subln=subln[l][None], w_out=w_out_c[l], norm2=norm2[l][None])
        mod_l = mod[l]
        rkv, qkv, lora = _in_proj(x, mod_l, norm1[l][None], w_in_r[l], geom)

        at, bt, kt, rt, v, pc, bonus, g = _rwkv_prep(rkv, lora, lp, geom)
        lmat = _lmat(at, bt, geom)
        tsys = _tri_inv(_to_systems(lmat, nchunks, heads))
        tmat = _from_systems(tsys, nchunks, heads)
        ys, fins = [], []
        for dr in range(2):
            s_lat = jnp.transpose(state_rwkv[:, l, dr], (0, 2, 1, 3)).reshape(dec_batch, RWKV_HEAD, rw)
            s0 = jnp.concatenate([jnp.zeros((batch, RWKV_HEAD, rw), F32), s_lat], axis=0)
            y_d, fin = _scan(at, bt, kt, rt, tmat, v, pc, s0, dr, geom)
            ys.append(y_d)
            fins.append(jnp.transpose(fin[:batch].reshape(batch, RWKV_HEAD, heads, RWKV_HEAD),
                                      (0, 2, 1, 3)))
        new_s.append(jnp.stack(fins, axis=1))

        lam_init = 0.8 - 0.6 * math.exp(-0.3 * l)
        o_ctx = _attention(qkv, 0, batch, seq, None, None, None, lp, lam_init, geom)
        ck = cache_attn_k[:, l].reshape(dec_batch, past, aw)
        cv = cache_attn_v[:, l].reshape(dec_batch, past, aw)
        o_lat = _attention(qkv, nctx, dec_batch, dec_seq, ck, cv, rope_tabs, lp, lam_init, geom)
        o_att = jnp.concatenate([o_ctx, o_lat], axis=0)
        new_k.append(qkv[:nctx, aw:2 * aw].reshape(batch, seq, aw // (2 * ATT_HEAD), 2 * ATT_HEAD))
        new_v.append(qkv[:nctx, 2 * aw:].reshape(batch, seq, aw // (2 * ATT_HEAD), 2 * ATT_HEAD))

        x, h2 = _mix_out(ys[0], ys[1], bonus, g, o_att, x, mod_l, lp, geom)
        x, y_final = _moe(h2, x, mod_l, router_w_p, router_b_p, wg[l], wu[l], wd[l],
                          norm_f[None], geom)

    y_prompt = y_final[:nctx].reshape(batch, seq, d)
    y_sample = y_final[nctx:].reshape(dec_batch, dec_seq, d)
    return (y_prompt, y_sample, jnp.stack(new_k, axis=1), jnp.stack(new_v, axis=1),
            jnp.stack(new_s, axis=1))
```

```python
import functools
import math

import jax
import jax.numpy as jnp
from jax import lax
from jax.experimental import pallas as pl
from jax.experimental.pallas import tpu as pltpu

F32 = jnp.float32
MXU_DTYPE = jnp.bfloat16
HI = lax.Precision.HIGHEST

GRID_W = 64
RWKV_HEAD = 64
DECAY_LORA = 64
AAA_LORA = 64
GATE_LORA = 160
ATT_HEAD = 64
ROPE_AXIS = ATT_HEAD // 2
ROPE_THETA = 10000.0
N_GROUPS = 4
TOP_K = 2
NORM_EPS = 1e-6
GN_EPS = 64e-5

CHUNK = 64
TM = 512
LANE = 128
VMEM_LIMIT = 56 * 1024 * 1024


def _cparams(*sem):
    return pltpu.CompilerParams(dimension_semantics=sem, vmem_limit_bytes=VMEM_LIMIT)


def _sigmoid(x):
    return 1.0 / (1.0 + jnp.exp(-x))


def _dot(a, b):
    return jnp.dot(a, b, preferred_element_type=F32)


def _dot_hi(a, b):
    return jnp.dot(a, b, precision=HI, preferred_element_type=F32)


def _dot_nt(a, b, precision=None):
    return lax.dot_general(a, b, (((1,), (1,)), ((), ())), precision=precision,
                           preferred_element_type=F32)


def _dot_exact_rhs(x, m_bf16, passes):
    acc = None
    rem = x
    for _ in range(passes):
        piece = rem.astype(jnp.bfloat16)
        term = jnp.dot(piece, m_bf16, preferred_element_type=F32)
        acc = term if acc is None else acc + term
        rem = rem - piece.astype(F32)
    return acc


def _dot_exact_lhs(m_bf16, x, passes):
    acc = None
    rem = x
    for _ in range(passes):
        piece = rem.astype(jnp.bfloat16)
        term = jnp.dot(m_bf16, piece, preferred_element_type=F32)
        acc = term if acc is None else acc + term
        rem = rem - piece.astype(F32)
    return acc


def _head_sum_matrix(width, head):
    r = lax.broadcasted_iota(jnp.int32, (width, width), 0) // head
    c = lax.broadcasted_iota(jnp.int32, (width, width), 1) // head
    return jnp.where(r == c, 1.0, 0.0).astype(jnp.bfloat16)


def _mod_kernel(c_ref, w_ref, b_ref, o_ref):
    c = c_ref[...]
    s = c * _sigmoid(c)
    o_ref[0] = _dot_hi(s, w_ref[0]) + b_ref[0]


def _modulation(cvec, w_mod, b_mod):
    depth, d, six_d = w_mod.shape
    tn = six_d // 4
    return pl.pallas_call(
        _mod_kernel,
        grid=(depth, six_d // tn),
        in_specs=[pl.BlockSpec((8, d), lambda l, j: (0, 0)),
                  pl.BlockSpec((1, d, tn), lambda l, j: (l, 0, j)),
                  pl.BlockSpec((1, 1, tn), lambda l, j: (l, 0, j))],
        out_specs=pl.BlockSpec((1, 8, tn), lambda l, j: (l, 0, j)),
        out_shape=jax.ShapeDtypeStruct((depth, 8, six_d), F32),
        compiler_params=_cparams("parallel", "parallel"),
        name="modulation",
    )(cvec, w_mod, b_mod.reshape(depth, 1, six_d))


def _in_proj_kernel(x_ref, mod_ref, g_ref, w_ref, rkv_ref, qkv_ref, lora_ref, *, d, rw):
    x = x_ref[...]
    ms = jnp.mean(x * x, axis=-1, keepdims=True)
    y = x * lax.rsqrt(ms + NORM_EPS) * g_ref[...]
    m = mod_ref[0]
    h = y * (1.0 + m[:, d:2 * d]) + m[:, 0:d]
    p = _dot(h.astype(MXU_DTYPE), w_ref[...])
    rkv_ref[...] = p[:, :3 * rw]
    qkv_ref[...] = p[:, 3 * rw:6 * rw]
    lora_ref[...] = p[:, 6 * rw:]


def _in_proj(x, mod_l, norm_g, w_in_r, geom):
    n, d = x.shape
    rw = geom["rw"]
    lw = w_in_r.shape[1] - 6 * rw
    row_of = geom["mod_row_of_tile"]
    return pl.pallas_call(
        functools.partial(_in_proj_kernel, d=d, rw=rw),
        grid=(n // TM,),
        in_specs=[pl.BlockSpec((TM, d), lambda i: (i, 0)),
                  pl.BlockSpec((1, 1, 6 * d), lambda i: (row_of(i), 0, 0)),
                  pl.BlockSpec((1, d), lambda i: (0, 0)),
                  pl.BlockSpec(w_in_r.shape, lambda i: (0, 0))],
        out_specs=[pl.BlockSpec((TM, 3 * rw), lambda i: (i, 0)),
                   pl.BlockSpec((TM, 3 * rw), lambda i: (i, 0)),
                   pl.BlockSpec((TM, lw), lambda i: (i, 0))],
        out_shape=[jax.ShapeDtypeStruct((n, 3 * rw), F32),
                   jax.ShapeDtypeStruct((n, 3 * rw), F32),
                   jax.ShapeDtypeStruct((n, lw), F32)],
        compiler_params=_cparams("parallel"),
        name="in_proj",
    )(x, mod_l, norm_g, w_in_r)


def _rwkv_prep_kernel(rkv_ref, prev_ref, next_ref, lora_ref, shw_ref, kk_ref, ka_ref, rk_ref,
                      w0_ref, wup_ref, a0_ref, aup_ref, gup_ref,
                      at_ref, bt_ref, kt_ref, rt_ref, v_ref, pc_ref, bonus_ref, g_ref,
                      cl_ref, *, rw, nctx, seq, dec_seq):
    i = pl.program_id(0)
    x = rkv_ref[...]
    row = i * TM + lax.broadcasted_iota(jnp.int32, (TM, 1), 0)
    is_ctx = row < nctx
    pos_c = row % seq
    pos_l = (row - nctx) % dec_seq
    pos = jnp.where(is_ctx, pos_c, pos_l)
    first = pos == 0
    last = pos == jnp.where(is_ctx, seq - 1, dec_seq - 1)
    loc = lax.broadcasted_iota(jnp.int32, (TM, 1), 0)
    xp = pltpu.roll(x, 1, 0)
    xp = jnp.where(loc == 0, prev_ref[7:8, :], xp)
    xp = jnp.where(first, 0.0, xp)
    xn = pltpu.roll(x, TM - 1, 0)
    xn = jnp.where(loc == TM - 1, next_ref[0:1, :], xn)
    xn = jnp.where(last, 0.0, xn)
    xs = shw_ref[0:1, :] * xp + shw_ref[1:2, :] * x + shw_ref[2:3, :] * xn
    r = xs[:, :rw]
    k = xs[:, rw:2 * rw]
    v = xs[:, 2 * rw:]
    hs = _head_sum_matrix(rw, RWKV_HEAD)

    kk = k * kk_ref[...]
    kk = kk * lax.rsqrt(_dot_exact_rhs(kk * kk, hs, 3) + 1e-12)

    lora = lora_ref[...]
    gd = lora[:, 2 * DECAY_LORA + 2 * AAA_LORA:]
    g_ref[...] = _dot(_sigmoid(gd).astype(MXU_DTYPE), gup_ref[...])
    v_ref[...] = v

    ti = lax.broadcasted_iota(jnp.int32, (CHUNK, CHUNK), 0)
    tj = lax.broadcasted_iota(jnp.int32, (CHUNK, CHUNK), 1)
    nchunk = TM // CHUNK
    ci = lax.broadcasted_iota(jnp.int32, (nchunk, TM), 0)
    cj = lax.broadcasted_iota(jnp.int32, (nchunk, TM), 1) // CHUNK
    chunk_sel = jnp.where(ci == cj, 1.0, 0.0).astype(jnp.bfloat16)

    bonus = jnp.zeros((TM, rw), F32)
    for dr in range(2):
        wd = lora[:, dr * DECAY_LORA:(dr + 1) * DECAY_LORA]
        ad = lora[:, 2 * DECAY_LORA + dr * AAA_LORA:2 * DECAY_LORA + (dr + 1) * AAA_LORA]
        w_pre = w0_ref[dr:dr + 1, :] + _dot_hi(jnp.tanh(wd), wup_ref[dr])
        z = -w_pre
        softplus = jnp.maximum(z, 0.0) + jnp.log(1.0 + jnp.exp(-jnp.abs(z)))
        logw = -jnp.exp(-softplus - 0.5)
        a = _sigmoid(a0_ref[dr:dr + 1, :] + _dot_hi(ad, aup_ref[dr]))
        k_d = k * (1.0 + (a - 1.0) * ka_ref[...])
        bonus = bonus + _dot_exact_rhs(r * k_d * rk_ref[...], hs, 3) * v
        tri = jnp.where(tj <= ti if dr == 0 else tj >= ti, 1.0, 0.0).astype(jnp.bfloat16)
        for c in range(nchunk):
            sl = pl.ds(c * CHUNK, CHUNK)
            cl_ref[sl, :] = _dot_exact_lhs(tri, logw[c * CHUNK:(c + 1) * CHUNK, :], 3)
        cl = cl_ref[...]
        e_pos = jnp.exp(cl)
        e_neg = jnp.exp(-cl)
        at_ref[dr] = jnp.exp(cl - logw) * (-kk)
        bt_ref[dr] = kk * a * e_neg
        kt_ref[dr] = k_d * e_neg
        rt_ref[dr] = r * e_pos
        pc_ref[dr] = jnp.exp(_dot_exact_lhs(chunk_sel, logw, 3))
    bonus_ref[...] = bonus


def _rwkv_prep(rkv, lora, lp, geom):
    n, w3 = rkv.shape
    rw = geom["rw"]
    lw = lora.shape[1]
    nb8 = n // 8
    t8 = TM // 8
    tok = lambda i: (i, 0)
    const2 = lambda i: (0, 0)
    const3 = lambda i: (0, 0, 0)
    dir_tok = lambda i: (0, i, 0)
    kern = functools.partial(_rwkv_prep_kernel, rw=rw, nctx=geom["nctx"], seq=geom["seq"],
                             dec_seq=geom["dec_seq"])
    outs = pl.pallas_call(
        kern,
        grid=(n // TM,),
        in_specs=[pl.BlockSpec((TM, w3), tok),
                  pl.BlockSpec((8, w3), lambda i: (jnp.maximum(i * t8 - 1, 0), 0)),
                  pl.BlockSpec((8, w3), lambda i: (jnp.minimum((i + 1) * t8, nb8 - 1), 0)),
                  pl.BlockSpec((TM, lw), tok),
                  pl.BlockSpec((3, w3), const2),
                  pl.BlockSpec((1, rw), const2),
                  pl.BlockSpec((1, rw), const2),
                  pl.BlockSpec((1, rw), const2),
                  pl.BlockSpec((2, rw), const2),
                  pl.BlockSpec((2, DECAY_LORA, rw), const3),
                  pl.BlockSpec((2, rw), const2),
                  pl.BlockSpec((2, AAA_LORA, rw), const3),
                  pl.BlockSpec(lp["g_up"].shape, const2)],
        out_specs=[pl.BlockSpec((2, TM, rw), dir_tok)] * 4
                  + [pl.BlockSpec((TM, rw), tok),
                     pl.BlockSpec((2, TM // CHUNK, rw), dir_tok),
                     pl.BlockSpec((TM, rw), tok),
                     pl.BlockSpec((TM, rw), tok)],
        out_shape=[jax.ShapeDtypeStruct((2, n, rw), F32)] * 4
                  + [jax.ShapeDtypeStruct((n, rw), F32),
                     jax.ShapeDtypeStruct((2, n // CHUNK, rw), F32),
                     jax.ShapeDtypeStruct((n, rw), F32),
                     jax.ShapeDtypeStruct((n, rw), F32)],
        scratch_shapes=[pltpu.VMEM((TM, rw), F32)],
        compiler_params=_cparams("parallel"),
        name="rwkv_prep",
    )(rkv, rkv, rkv, lora, lp["shift_w"], lp["k_k"], lp["k_a"], lp["r_k"], lp["w0"], lp["w_up"],
      lp["a0"], lp["a_up"], lp["g_up"])
    return outs


def _tri_masks(reverse):
    ti = lax.broadcasted_iota(jnp.int32, (CHUNK, CHUNK), 0)
    tj = lax.broadcasted_iota(jnp.int32, (CHUNK, CHUNK), 1)
    if reverse:
        return tj > ti, tj >= ti
    return tj < ti, tj <= ti


def _lmat_kernel(at_ref, bt_ref, l_ref, *, rw):
    nchunk = TM // CHUNK
    heads = rw // RWKV_HEAD
    for dr in range(2):
        strict, _ = _tri_masks(dr == 1)
        for c in range(nchunk):
            rs = slice(c * CHUNK, (c + 1) * CHUNK)
            for h in range(heads):
                ls = slice(h * RWKV_HEAD, (h + 1) * RWKV_HEAD)
                m = _dot_nt(at_ref[dr, rs, ls].astype(MXU_DTYPE), bt_ref[dr, rs, ls].astype(MXU_DTYPE))
                l_ref[dr, rs, ls] = jnp.where(strict, m, 0.0)


def _lmat(at, bt, geom):
    _, n, rw = at.shape
    spec = pl.BlockSpec((2, TM, rw), lambda i: (0, i, 0))
    return pl.pallas_call(
        functools.partial(_lmat_kernel, rw=rw),
        grid=(n // TM,),
        in_specs=[spec, spec],
        out_specs=spec,
        out_shape=jax.ShapeDtypeStruct((2, n, rw), F32),
        compiler_params=_cparams("parallel"),
        name="scan_lmat",
    )(at, bt)


def _tri_inv_kernel(l_ref, t_ref):
    jj = lax.broadcasted_iota(jnp.int32, (CHUNK, LANE), 0)

    def row(t, carry):
        def col(i, acc):
            return acc + l_ref[t, pl.ds(i, 1), :] * t_ref[i]
        acc = lax.fori_loop(0, t, col, jnp.where(jj == t, 1.0, 0.0))
        t_ref[t] = acc
        return carry

    lax.fori_loop(0, CHUNK, row, 0)


def _tri_inv(lt):
    nsys = lt.shape[2]
    spec = pl.BlockSpec((CHUNK, CHUNK, LANE), lambda i: (0, 0, i))
    return pl.pallas_call(
        _tri_inv_kernel,
        grid=(nsys // LANE,),
        in_specs=[spec],
        out_specs=spec,
        out_shape=jax.ShapeDtypeStruct(lt.shape, F32),
        compiler_params=_cparams("parallel"),
        name="scan_tri_inv",
    )(lt)


def _scan_kernel(at_ref, bt_ref, kt_ref, rt_ref, tm_ref, v_ref, pc_ref, s0_ref,
                 y_ref, fin_ref, s_ref, vt_ref, yt_ref, *, rw, reverse, geom):
    j = pl.program_id(0)
    ntile = pl.num_programs(0)
    tile = (ntile - 1 - j) if reverse else j
    nchunk = TM // CHUNK
    heads = rw // RWKV_HEAD
    _, incl = _tri_masks(reverse)
    ti2 = lax.broadcasted_iota(jnp.int32, (2 * CHUNK, CHUNK), 0)
    tj2 = lax.broadcasted_iota(jnp.int32, (2 * CHUNK, CHUNK), 1)
    is_r = ti2 >= CHUNK
    tl2 = jnp.where(is_r, ti2 - CHUNK, ti2)
    mask2 = ((tj2 > tl2) if reverse else (tj2 < tl2)) | (is_r & (tj2 == tl2))
    cchunks = geom["nctx"] // CHUNK
    seq_chunks = geom["seq"] // CHUNK
    dec_chunks = geom["dec_seq"] // CHUNK
    bf = MXU_DTYPE

    vt_ref[...] = v_ref[...].T
    order = range(nchunk - 1, -1, -1) if reverse else range(nchunk)
    for c in order:
        gch = tile * nchunk + c
        in_ctx = gch < cchunks
        lat = gch - cchunks
        seq_id = jnp.where(in_ctx, gch // seq_chunks, geom["batch"] + lat // dec_chunks)
        local = jnp.where(in_ctx, gch % seq_chunks, lat % dec_chunks)
        length = jnp.where(in_ctx, seq_chunks, dec_chunks)
        starts = (local == length - 1) if reverse else (local == 0)
        ends = (local == 0) if reverse else (local == length - 1)

        @pl.when(starts)
        def _():
            s_ref[...] = s0_ref[seq_id]

        rs = slice(c * CHUNK, (c + 1) * CHUNK)
        for h in range(heads):
            ls = slice(h * RWKV_HEAD, (h + 1) * RWKV_HEAD)
            a = at_ref[rs, ls].astype(bf)
            b = bt_ref[rs, ls].astype(bf)
            k = kt_ref[rs, ls].astype(bf)
            r = rt_ref[rs, ls].astype(bf)
            tmat = tm_ref[rs, ls].astype(bf)
            vt = vt_ref[ls, rs].astype(bf)
            s = s_ref[:, ls]
            ark2 = jnp.where(mask2, _dot_nt(jnp.concatenate([a, r], axis=0), k), 0.0).astype(bf)
            aak = ark2[:CHUNK]
            ark = ark2[CHUNK:]
            arb = jnp.where(incl, _dot_nt(r, b), 0.0).astype(bf)
            x2t = _dot_nt(vt, aak).astype(bf)
            w = _dot(tmat, a).astype(bf)
            sw = _dot_nt(s.astype(bf), jnp.concatenate([w, r], axis=0))
            ut = sw[:, :CHUNK] + _dot_nt(x2t, tmat)
            uv = jnp.concatenate([ut.astype(bf), vt], axis=1)
            yt_ref[ls, rs] = sw[:, CHUNK:] + _dot_nt(uv, jnp.concatenate([arb, ark], axis=1))
            s_new = (s + _dot(uv, jnp.concatenate([b, k], axis=0))) * pc_ref[c:c + 1, ls]
            s_ref[:, ls] = s_new

        @pl.when(ends)
        def _():
            fin_ref[seq_id] = s_ref[...]

    y_ref[...] = yt_ref[...].T


def _scan(at, bt, kt, rt, tmat, v, pc, s0, dr, geom):
    n, rw = v.shape
    ntile = n // TM
    reverse = dr == 1
    nseq = s0.shape[0]
    if reverse:
        tix = lambda j: (dr, ntile - 1 - j, 0)
        vix = lambda j: (ntile - 1 - j, 0)
    else:
        tix = lambda j: (dr, j, 0)
        vix = lambda j: (j, 0)
    dspec = pl.BlockSpec((None, TM, rw), tix)
    kern = functools.partial(_scan_kernel, rw=rw, reverse=reverse, geom=geom)
    return pl.pallas_call(
        kern,
        grid=(ntile,),
        in_specs=[dspec, dspec, dspec, dspec, dspec,
                  pl.BlockSpec((TM, rw), vix),
                  pl.BlockSpec((None, TM // CHUNK, rw), tix),
                  pl.BlockSpec(s0.shape, lambda j: (0, 0, 0))],
        out_specs=[pl.BlockSpec((TM, rw), vix),
                   pl.BlockSpec(s0.shape, lambda j: (0, 0, 0))],
        out_shape=[jax.ShapeDtypeStruct((n, rw), F32),
                   jax.ShapeDtypeStruct(s0.shape, F32)],
        scratch_shapes=[pltpu.VMEM((RWKV_HEAD, rw), F32),
                        pltpu.VMEM((rw, TM), F32),
                        pltpu.VMEM((rw, TM), F32)],
        compiler_params=_cparams("arbitrary"),
        name="scan_seq_rev" if reverse else "scan_seq_fwd",
    )(at, bt, kt, rt, tmat, v, pc, s0)


def _rope(x, cos, sin_signed):
    lane = lax.broadcasted_iota(jnp.int32, x.shape, 1)
    half = ROPE_AXIS // 2
    swapped = jnp.where(lane % ROPE_AXIS < half, pltpu.roll(x, LANE - half, 1), pltpu.roll(x, half, 1))
    return x * cos + swapped * sin_signed


def _attn_kernel(*refs, t_own, past, use_rope, lam_init, tq):
    if use_rope:
        (q_ref, k_ref, v_ref, ck_ref, cv_ref, cosq_ref, sinq_ref, cosk_ref, sink_ref,
         lq1_ref, lk1_ref, lq2_ref, lk2_ref, sub_ref, o_ref, kall_ref, vall_ref) = refs
    else:
        (q_ref, k_ref, v_ref, lq1_ref, lk1_ref, lq2_ref, lk2_ref, sub_ref,
         o_ref, kall_ref, vall_ref) = refs
    i = pl.program_id(2)

    @pl.when(i == 0)
    def _():
        k = k_ref[...]
        if use_rope:
            k = _rope(k, cosk_ref[...], sink_ref[...])
        kall_ref[0:t_own, :] = k.astype(MXU_DTYPE)
        vall_ref[0:t_own, :] = v_ref[...].astype(MXU_DTYPE)
        if past:
            kall_ref[t_own:t_own + past, :] = ck_ref[...].astype(MXU_DTYPE)
            vall_ref[t_own:t_own + past, :] = cv_ref[...].astype(MXU_DTYPE)

    lam = (jnp.exp(jnp.sum(lq1_ref[...] * lk1_ref[...], axis=-1, keepdims=True))
           - jnp.exp(jnp.sum(lq2_ref[...] * lk2_ref[...], axis=-1, keepdims=True)) + lam_init)
    q = q_ref[...]
    if use_rope:
        q = _rope(q, cosq_ref[...], sinq_ref[...])
    q = q * (ATT_HEAD ** -0.5)
    lane = lax.broadcasted_iota(jnp.int32, q.shape, 1)
    q1 = jnp.where(lane < ATT_HEAD, q, 0.0).astype(MXU_DTYPE)
    q2 = jnp.where(lane >= ATT_HEAD, q, 0.0).astype(MXU_DTYPE)
    kall = kall_ref[...]
    s1 = _dot_nt(q1, kall)
    s2 = _dot_nt(q2, kall)
    e1 = jnp.exp(s1 - jnp.max(s1, axis=-1, keepdims=True))
    e2 = jnp.exp(s2 - jnp.max(s2, axis=-1, keepdims=True))
    inv1 = 1.0 / jnp.sum(e1, axis=-1, keepdims=True)
    inv2 = lam / jnp.sum(e2, axis=-1, keepdims=True)
    p = e1 * inv1 - e2 * inv2
    o = _dot(p.astype(MXU_DTYPE), vall_ref[...])
    o = o * lax.rsqrt(jnp.mean(o * o, axis=-1, keepdims=True) + NORM_EPS) * sub_ref[...]
    o_ref[...] = o * (1.0 - lam_init)


def _attention(qkv, row_off, nbatch, t_own, ctx_k, ctx_v, rope_tabs, lp, lam_init, geom):
    n, w3 = qkv.shape
    aw = w3 // 3
    hd2 = 2 * ATT_HEAD
    nh = aw // hd2
    tq = min(256, t_own)
    nq = t_own // tq
    past = 0 if ctx_k is None else ctx_k.shape[1]
    use_rope = rope_tabs is not None
    qoff = row_off // tq
    koff = row_off // t_own
    in_specs = [pl.BlockSpec((tq, hd2), lambda b, h, i: (qoff + b * nq + i, h)),
                pl.BlockSpec((t_own, hd2), lambda b, h, i: (koff + b, nh + h)),
                pl.BlockSpec((t_own, hd2), lambda b, h, i: (koff + b, 2 * nh + h))]
    args = [qkv, qkv, qkv]
    if use_rope:
        cos_t, sin_t = rope_tabs
        in_specs += [pl.BlockSpec((None, past, hd2), lambda b, h, i: (b, 0, h)),
                     pl.BlockSpec((None, past, hd2), lambda b, h, i: (b, 0, h)),
                     pl.BlockSpec((tq, hd2), lambda b, h, i: (i, 0)),
                     pl.BlockSpec((tq, hd2), lambda b, h, i: (i, 0)),
                     pl.BlockSpec((t_own, hd2), lambda b, h, i: (0, 0)),
                     pl.BlockSpec((t_own, hd2), lambda b, h, i: (0, 0))]
        args += [ctx_k, ctx_v, cos_t, sin_t, cos_t, sin_t]
    small = pl.BlockSpec((1, ATT_HEAD), lambda b, h, i: (0, 0))
    in_specs += [small, small, small, small, pl.BlockSpec((1, hd2), lambda b, h, i: (0, 0))]
    args += [lp["lambda_q1"], lp["lambda_k1"], lp["lambda_q2"], lp["lambda_k2"], lp["subln"]]
    kern = functools.partial(_attn_kernel, t_own=t_own, past=past, use_rope=use_rope,
                             lam_init=lam_init, tq=tq)
    return pl.pallas_call(
        kern,
        grid=(nbatch, nh, nq),
        in_specs=in_specs,
        out_specs=pl.BlockSpec((tq, hd2), lambda b, h, i: (b * nq + i, h)),
        out_shape=jax.ShapeDtypeStruct((nbatch * t_own, aw), F32),
        scratch_shapes=[pltpu.VMEM((t_own + past, hd2), MXU_DTYPE),
                        pltpu.VMEM((t_own + past, hd2), MXU_DTYPE)],
        compiler_params=_cparams("parallel", "parallel", "arbitrary"),
        name="diff_attn_lat" if use_rope else "diff_attn_ctx",
    )(*args)


def _mix_out_kernel(y0_ref, y1_ref, bonus_ref, g_ref, o_ref, x_ref, mod_ref, lnw_ref, lnb_ref,
                    wout_ref, n2_ref, xo_ref, h2_ref, *, d, rw):
    y = y0_ref[...] + y1_ref[...]
    hs = _head_sum_matrix(rw, RWKV_HEAD)
    inv = 1.0 / RWKV_HEAD
    mu = _dot_exact_rhs(y, hs, 3) * inv
    yc = y - mu
    var = _dot_exact_rhs(yc * yc, hs, 3) * inv
    yn = yc * lax.rsqrt(var + GN_EPS)
    y_rwkv = (yn * lnw_ref[...] + lnb_ref[...] + bonus_ref[...]) * g_ref[...]
    mix = (_dot(y_rwkv.astype(MXU_DTYPE), wout_ref[0:rw, :])
           + _dot(o_ref[...].astype(MXU_DTYPE), wout_ref[rw:, :]))
    m = mod_ref[0]
    x = x_ref[...] + m[:, 2 * d:3 * d] * mix
    xo_ref[...] = x
    hn = x * lax.rsqrt(jnp.mean(x * x, axis=-1, keepdims=True) + NORM_EPS) * n2_ref[...]
    h2_ref[...] = hn * (1.0 + m[:, 4 * d:5 * d]) + m[:, 3 * d:4 * d]


def _mix_out(y0, y1, bonus, g, o_att, x, mod_l, lp, geom):
    n, d = x.shape
    rw = geom["rw"]
    aw = o_att.shape[1]
    row_of = geom["mod_row_of_tile"]
    tok = lambda i: (i, 0)
    const = lambda i: (0, 0)
    return pl.pallas_call(
        functools.partial(_mix_out_kernel, d=d, rw=rw),
        grid=(n // TM,),
        in_specs=[pl.BlockSpec((TM, rw), tok), pl.BlockSpec((TM, rw), tok),
                  pl.BlockSpec((TM, rw), tok), pl.BlockSpec((TM, rw), tok),
                  pl.BlockSpec((TM, aw), tok), pl.BlockSpec((TM, d), tok),
                  pl.BlockSpec((1, 1, 6 * d), lambda i: (row_of(i), 0, 0)),
                  pl.BlockSpec((1, rw), const), pl.BlockSpec((1, rw), const),
                  pl.BlockSpec((rw + aw, d), const), pl.BlockSpec((1, d), const)],
        out_specs=[pl.BlockSpec((TM, d), tok), pl.BlockSpec((TM, d), tok)],
        out_shape=[jax.ShapeDtypeStruct((n, d), F32), jax.ShapeDtypeStruct((n, d), F32)],
        compiler_params=_cparams("parallel"),
        name="mix_out",
    )(y0, y1, bonus, g, o_att, x, mod_l, lp["ln_x_w"], lp["ln_x_b"], lp["w_out"], lp["norm2"])


def _route(h2, rw_ref, rb_ref, n_exp):
    logits = _dot_hi(h2, rw_ref[...])
    aff = _sigmoid(logits)
    sel = aff + rb_ref[...]
    per_group = n_exp // N_GROUPS
    col = [sel[:, e:e + 1] for e in range(n_exp)]
    affc = [aff[:, e:e + 1] for e in range(n_exp)]
    scores = []
    for gi in range(N_GROUPS):
        a, b, c, dd = col[gi * per_group:(gi + 1) * per_group]
        hi1, lo1 = jnp.maximum(a, b), jnp.minimum(a, b)
        hi2, lo2 = jnp.maximum(c, dd), jnp.minimum(c, dd)
        scores.append(jnp.maximum(hi1, hi2) + jnp.maximum(jnp.minimum(hi1, hi2), jnp.maximum(lo1, lo2)))
    best = jnp.zeros_like(scores[0], dtype=jnp.int32)
    bs = scores[0]
    for gi in range(1, N_GROUPS):
        better = scores[gi] > bs
        best = jnp.where(better, gi, best)
        bs = jnp.where(better, scores[gi], bs)
    neg = jnp.full_like(col[0], -jnp.inf)
    masked = [jnp.where(best == (e // per_group), col[e], neg) for e in range(n_exp)]
    picks = []
    for _ in range(TOP_K):
        idx = jnp.zeros_like(best)
        val = masked[0]
        for e in range(1, n_exp):
            better = masked[e] > val
            idx = jnp.where(better, e, idx)
            val = jnp.where(better, masked[e], val)
        picks.append(idx)
        masked = [jnp.where(idx == e, neg, masked[e]) for e in range(n_exp)]
    wts = []
    for idx in picks:
        w = jnp.zeros_like(col[0])
        for e in range(n_exp):
            w = jnp.where(idx == e, affc[e], w)
        wts.append(w)
    total = wts[0] + wts[1]
    lane = lax.broadcasted_iota(jnp.int32, sel.shape, 1)
    combine = jnp.zeros_like(sel)
    for idx, w in zip(picks, wts):
        combine = combine + jnp.where(lane == idx, w / total, 0.0)
    return combine


def _moe_kernel(h2_ref, x_ref, mod_ref, rw_ref, rb_ref, wg_ref, wu_ref, wd_ref, nf_ref,
                xo_ref, yf_ref, comb_ref, acc_ref, hb_ref, *, d, n_exp):
    e = pl.program_id(1)

    @pl.when(e == 0)
    def _():
        h2 = h2_ref[...]
        comb_ref[...] = _route(h2, rw_ref, rb_ref, n_exp)
        hb_ref[...] = h2.astype(MXU_DTYPE)
        acc_ref[...] = jnp.zeros_like(acc_ref)

    hb = hb_ref[...]
    hg = _dot(hb, wg_ref[0])
    hu = _dot(hb, wu_ref[0])
    comb = comb_ref[...]
    lane = lax.broadcasted_iota(jnp.int32, comb.shape, 1)
    ce = jnp.sum(jnp.where(lane == e, comb, 0.0), axis=-1, keepdims=True)
    act = hg * _sigmoid(hg) * hu * ce
    acc_ref[...] += _dot(act.astype(MXU_DTYPE), wd_ref[0])

    @pl.when(e == n_exp - 1)
    def _():
        m = mod_ref[0]
        x = x_ref[...] + m[:, 5 * d:6 * d] * acc_ref[...]
        xo_ref[...] = x
        yf_ref[...] = x * lax.rsqrt(jnp.mean(x * x, axis=-1, keepdims=True) + NORM_EPS) * nf_ref[...]


def _moe(h2, x, mod_l, router_w_p, router_b_p, wg, wu, wd, norm_f, geom):
    n, d = x.shape
    n_exp, _, ff = wg.shape
    row_of = geom["mod_row_of_tile"]
    tok = lambda i, e: (i, 0)
    const = lambda i, e: (0, 0)
    return pl.pallas_call(
        functools.partial(_moe_kernel, d=d, n_exp=n_exp),
        grid=(n // TM, n_exp),
        in_specs=[pl.BlockSpec((TM, d), tok), pl.BlockSpec((TM, d), tok),
                  pl.BlockSpec((1, 1, 6 * d), lambda i, e: (row_of(i), 0, 0)),
                  pl.BlockSpec((d, LANE), const), pl.BlockSpec((1, LANE), const),
                  pl.BlockSpec((1, d, ff), lambda i, e: (e, 0, 0)),
                  pl.BlockSpec((1, d, ff), lambda i, e: (e, 0, 0)),
                  pl.BlockSpec((1, ff, d), lambda i, e: (e, 0, 0)),
                  pl.BlockSpec((1, d), const)],
        out_specs=[pl.BlockSpec((TM, d), tok), pl.BlockSpec((TM, d), tok)],
        out_shape=[jax.ShapeDtypeStruct((n, d), F32), jax.ShapeDtypeStruct((n, d), F32)],
        scratch_shapes=[pltpu.VMEM((TM, LANE), F32), pltpu.VMEM((TM, d), F32),
                        pltpu.VMEM((TM, d), MXU_DTYPE)],
        compiler_params=_cparams("parallel", "arbitrary"),
        name="moe",
    )(h2, x, mod_l, router_w_p, router_b_p, wg, wu, wd, norm_f)


def _rope_tables(t):
    rows = t // GRID_W
    row = jnp.repeat(jnp.arange(rows), GRID_W)
    col = jnp.tile(jnp.arange(GRID_W), rows)
    inv = 1.0 / (ROPE_THETA ** (jnp.arange(0, ROPE_AXIS, 2, dtype=F32) / ROPE_AXIS))
    ang = jnp.concatenate([row[:, None] * inv, col[:, None] * inv], axis=-1)
    half = ROPE_AXIS // 2
    ang64 = jnp.concatenate([ang[:, :half], ang[:, :half], ang[:, half:], ang[:, half:]], axis=-1)
    sign = jnp.tile(jnp.concatenate([-jnp.ones((half,), F32), jnp.ones((half,), F32)]), 2)
    cos64 = jnp.cos(ang64)
    sin64 = jnp.sin(ang64) * sign
    return jnp.tile(cos64, (1, 2)), jnp.tile(sin64, (1, 2))


def _to_systems(lmat, nchunks, heads):
    l5 = lmat.reshape(2, nchunks, CHUNK, heads, RWKV_HEAD)
    fwd = jnp.transpose(l5[0], (1, 3, 0, 2))
    bwd = jnp.transpose(l5[1], (3, 1, 0, 2))
    return jnp.stack([fwd, bwd], axis=2).reshape(CHUNK, CHUNK, 2 * nchunks * heads)


def _from_systems(tsys, nchunks, heads):
    t5 = tsys.reshape(CHUNK, CHUNK, 2, nchunks, heads)
    fwd = jnp.transpose(t5[:, :, 0], (2, 0, 3, 1))
    bwd = jnp.transpose(t5[:, :, 1], (2, 1, 3, 0))
    return jnp.stack([fwd, bwd], axis=0).reshape(2, nchunks * CHUNK, heads * RWKV_HEAD)


def kernel(x_prompt, x_sample, cache_attn_k, cache_attn_v, state_rwkv, c, c_ctx, w_mod, b_mod, norm1, w_in, shift_w, k_k, k_a, r_k, w0, w_up, a0, a_up, g_up, ln_x_w, ln_x_b, lambda_q1, lambda_k1, lambda_q2, lambda_k2, subln, w_out, norm2, router_w, router_b, exp_gate, exp_up, exp_down, norm_f):
    batch, seq, d = x_prompt.shape
    dec_batch, dec_seq, _ = x_sample.shape
    depth = w_mod.shape[0]
    past = cache_attn_k.shape[2]
    rw = k_k.shape[1]
    heads = rw // RWKV_HEAD
    aw = w_out.shape[1] - rw
    n_exp = router_w.shape[1]
    nctx = batch * seq
    nlat = dec_batch * dec_seq
    n = nctx + nlat
    assert nctx % TM == 0 and dec_seq % TM == 0 and seq % CHUNK == 0 and nctx % dec_seq == 0
    assert dec_batch + 1 <= 8 and n_exp <= LANE and n_exp // N_GROUPS == 4 and aw == rw
    nchunks = n // CHUNK
    nseq = batch + dec_batch
    tiles_ctx = nctx // TM
    tiles_seq = dec_seq // TM

    def mod_row_of_tile(i):
        return jnp.where(i < tiles_ctx, 0, 1 + (i - tiles_ctx) // tiles_seq)

    geom = dict(rw=rw, nctx=nctx, seq=seq, dec_seq=dec_seq, batch=batch,
                mod_row_of_tile=mod_row_of_tile)

    cvec = jnp.zeros((8, d), F32).at[0].set(c_ctx).at[1:1 + dec_batch].set(c)
    mod = _modulation(cvec, w_mod, b_mod).reshape(depth, 8, 1, 6 * d)

    o_rkv = 3 * rw
    o_lora = o_rkv
    n_lora = 2 * DECAY_LORA + 2 * AAA_LORA + GATE_LORA
    lora_w = -(-n_lora // LANE) * LANE
    gate_w = lora_w - 2 * DECAY_LORA - 2 * AAA_LORA
    w_in_r = jnp.concatenate(
        [w_in[:, :, :o_rkv], w_in[:, :, o_lora + n_lora:], w_in[:, :, o_lora:o_lora + n_lora],
         jnp.zeros((depth, d, lora_w - n_lora), F32)], axis=-1).astype(MXU_DTYPE)
    g_up_p = jnp.concatenate([g_up, jnp.zeros((depth, gate_w - GATE_LORA, rw), F32)],
                             axis=1).astype(MXU_DTYPE)
    w_out_c = w_out.astype(MXU_DTYPE)
    wg = exp_gate.astype(MXU_DTYPE)
    wu = exp_up.astype(MXU_DTYPE)
    wd = exp_down.astype(MXU_DTYPE)
    router_w_p = jnp.zeros((d, LANE), F32).at[:, :n_exp].set(router_w)
    router_b_p = jnp.zeros((1, LANE), F32).at[0, :n_exp].set(router_b)
    rope_tabs = _rope_tables(dec_seq)

    x = jnp.concatenate([x_prompt.reshape(nctx, d), x_sample.reshape(nlat, d)], axis=0)
    new_k, new_v, new_s = [], [], []
    y_final = None
    for l in range(depth):
        lp = dict(shift_w=shift_w[l], k_k=k_k[l][None], k_a=k_a[l][None], r_k=r_k[l].reshape(1, rw),
                  w0=w0[l], w_up=w_up[l], a0=a0[l], a_up=a_up[l], g_up=g_up_p[l],
                  ln_x_w=ln_x_w[l][None], ln_x_b=ln_x_b[l][None],
                  lambda_q1=lambda_q1[l][None], lambda_k1=lambda_k1[l][None],
                  lambda_q2=lambda_q2[l][None], lambda_k2=lambda_k2[l][None],
                  subln=subln[l][None], w_out=w_out_c[l], norm2=norm2[l][None])
        mod_l = mod[l]
        rkv, qkv, lora = _in_proj(x, mod_l, norm1[l][None], w_in_r[l], geom)

        at, bt, kt, rt, v, pc, bonus, g = _rwkv_prep(rkv, lora, lp, geom)
        lmat = _lmat(at, bt, geom)
        tsys = _tri_inv(_to_systems(lmat, nchunks, heads))
        tmat = _from_systems(tsys, nchunks, heads)
        ys, fins = [], []
        for dr in range(2):
            s_lat = jnp.transpose(state_rwkv[:, l, dr], (0, 2, 1, 3)).reshape(dec_batch, RWKV_HEAD, rw)
            s0 = jnp.concatenate([jnp.zeros((batch, RWKV_HEAD, rw), F32), s_lat], axis=0)
            y_d, fin = _scan(at, bt, kt, rt, tmat, v, pc, s0, dr, geom)
            ys.append(y_d)
            fins.append(jnp.transpose(fin[:batch].reshape(batch, RWKV_HEAD, heads, RWKV_HEAD),
                                      (0, 2, 1, 3)))
        new_s.append(jnp.stack(fins, axis=1))

        lam_init = 0.8 - 0.6 * math.exp(-0.3 * l)
        o_ctx = _attention(qkv, 0, batch, seq, None, None, None, lp, lam_init, geom)
        ck = cache_attn_k[:, l].reshape(dec_batch, past, aw)
        cv = cache_attn_v[:, l].reshape(dec_batch, past, aw)
        o_lat = _attention(qkv, nctx, dec_batch, dec_seq, ck, cv, rope_tabs, lp, lam_init, geom)
        o_att = jnp.concatenate([o_ctx, o_lat], axis=0)
        new_k.append(qkv[:nctx, aw:2 * aw].reshape(batch, seq, aw // (2 * ATT_HEAD), 2 * ATT_HEAD))
        new_v.append(qkv[:nctx, 2 * aw:].reshape(batch, seq, aw // (2 * ATT_HEAD), 2 * ATT_HEAD))

        x, h2 = _mix_out(ys[0], ys[1], bonus, g, o_att, x, mod_l, lp, geom)
        x, y_final = _moe(h2, x, mod_l, router_w_p, router_b_p, wg[l], wu[l], wd[l],
                          norm_f[None], geom)

    y_prompt = y_final[:nctx].reshape(batch, seq, d)
    y_sample = y_final[nctx:].reshape(dec_batch, dec_seq, d)
    return (y_prompt, y_sample, jnp.stack(new_k, axis=1), jnp.stack(new_v, axis=1),
            jnp.stack(new_s, axis=1))
```
